```python
import math
import jax
import jax.numpy as jnp
from jax import lax
import numpy as np

D_MODEL = 1024
BATCH = 16
SEQ = 256
DEPTH = 2
DEC_BATCH = 4
DEC_SEQ = 2048
PAST_LEN = 512

GRID_W = 64
MIX_WIDTH = D_MODEL
POOL_WIDTH = MIX_WIDTH // 4
POOL_WINDOWS = (2, 4, 8, 16)
N_POOL_GROUPS = len(POOL_WINDOWS)
POOL_GROUP = POOL_WIDTH // N_POOL_GROUPS
NA_WIDTH = MIX_WIDTH // 2
NA_HEAD_DIM = 64
NA_HEADS = NA_WIDTH // NA_HEAD_DIM
NA_ROWS = 8
NA_COLS = 16
Q_BLOCK = 128
DN_WIDTH = MIX_WIDTH - POOL_WIDTH - NA_WIDTH
DN_HEAD_DIM = 64
DN_HEADS = DN_WIDTH // DN_HEAD_DIM
DN_CONV = 4
DN_CHUNK = 64
N_EXPERTS = 16
N_EXPERT_GROUPS = 4
EXPERTS_PER_GROUP = N_EXPERTS // N_EXPERT_GROUPS
TOP_K = 2
D_EXPERT = 512
IN_SIZES = (POOL_WIDTH, NA_WIDTH, NA_WIDTH, NA_WIDTH, 3 * DN_WIDTH, DN_WIDTH, 2 * DN_HEADS, 2 * DN_HEADS)
IN_WIDTH = sum(IN_SIZES)
EPS = 1e-6
F32 = jnp.float32

kernel_name = 'hybrid_pool_natten_deltanet_moe_diffusion_step'


def _rms_norm(x, g):
    xf = x.astype(F32)
    y = xf * lax.rsqrt(jnp.mean(xf * xf, axis=-1, keepdims=True) + EPS)
    return (y * g.astype(F32)).astype(x.dtype)


def _l2norm(x):
    return x * lax.rsqrt(jnp.sum(x * x, axis=-1, keepdims=True) + EPS)


def _heads(t, n):
    b, l, _ = t.shape
    return t.reshape(b, l, n, -1).transpose(0, 2, 1, 3)


def _split_cols(t, sizes):
    out, start = [], 0
    for s in sizes:
        out.append(t[..., start:start + s])
        start += s
    return out


def _multi_scale_pool(u):
    b, l, _ = u.shape
    uf = u.astype(F32)
    csum = jnp.pad(jnp.cumsum(uf, axis=1), ((0, 0), (1, 0), (0, 0)))
    t = jnp.arange(l)
    means = []
    for gi, win in enumerate(POOL_WINDOWS):
        lo = jnp.clip(t - win // 2, 0, l)
        hi = jnp.clip(t + win - win // 2, 0, l)
        cg = csum[:, :, gi * POOL_GROUP:(gi + 1) * POOL_GROUP]
        means.append((cg[:, hi] - cg[:, lo]) / (hi - lo).astype(F32)[None, :, None])
    return (jnp.concatenate(means, axis=-1) - uf).astype(u.dtype)


def _context_attention(q, k, v):
    b, h, s, d = q.shape
    nb = s // Q_BLOCK
    qb = q.reshape(b, h, nb, Q_BLOCK, d).transpose(2, 0, 1, 3, 4)
    scale = d ** -0.5

    def block(qi):
        sc = jnp.einsum('bhqd,bhkd->bhqk', qi, k).astype(F32) * scale
        p = jax.nn.softmax(sc, axis=-1).astype(v.dtype)
        return jnp.einsum('bhqk,bhkd->bhqd', p, v)

    o = lax.map(block, qb)
    return o.transpose(1, 2, 0, 3, 4).reshape(b, h, s, d)


def _neighbourhood_attention(q, k, v, k_ctx, v_ctx, rpb):
    b, h, l, d = q.shape
    rows = l // GRID_W
    kh = min(NA_ROWS, rows)
    n_loc = kh * GRID_W
    scale = d ** -0.5
    qg = q.reshape(b, h, rows, GRID_W, d)
    kg = k.reshape(b, h, rows, GRID_W, d)
    vg = v.reshape(b, h, rows, GRID_W, d)
    col = jnp.arange(GRID_W)
    col_start = jnp.clip(col - NA_COLS // 2, 0, GRID_W - NA_COLS)
    in_win = (col[None, :] >= col_start[:, None]) & (col[None, :] < col_start[:, None] + NA_COLS)
    dc_idx = jnp.clip(col[None, :] - col[:, None] + NA_COLS - 1, 0, 2 * NA_COLS - 2)
    rpb_cols = rpb.astype(F32)[:, :, dc_idx]

    def one_row(r):
        r0 = jnp.clip(r - kh // 2, 0, rows - kh)
        kb = lax.dynamic_slice_in_dim(kg, r0, kh, axis=2).reshape(b, h, n_loc, d)
        vb = lax.dynamic_slice_in_dim(vg, r0, kh, axis=2).reshape(b, h, n_loc, d)
        qr = lax.dynamic_index_in_dim(qg, r, axis=2, keepdims=False)
        dr_idx = r0 + jnp.arange(kh) - r + NA_ROWS - 1
        bias = jnp.take(rpb_cols, dr_idx, axis=1)
        bias = jnp.where(in_win[None, None], bias, -jnp.inf)
        bias = bias.transpose(0, 2, 1, 3).reshape(h, GRID_W, n_loc)
        s_loc = jnp.einsum('bhqd,bhkd->bhqk', qr, kb).astype(F32) * scale + bias
        s_ctx = jnp.einsum('bhqd,bhpd->bhqp', qr, k_ctx).astype(F32) * scale
        p = jax.nn.softmax(jnp.concatenate([s_loc, s_ctx], axis=-1), axis=-1).astype(v.dtype)
        return (jnp.einsum('bhqk,bhkd->bhqd', p[..., :n_loc], vb)
                + jnp.einsum('bhqp,bhpd->bhqd', p[..., n_loc:], v_ctx.astype(v.dtype)))

    o = lax.map(one_row, jnp.arange(rows))
    return o.transpose(1, 2, 0, 3, 4).reshape(b, h, l, d)


def _short_conv(u, w):
    kw, c = w.shape
    lo = (kw - 1) // 2
    return lax.conv_general_dilated(u, w[:, None, :].astype(u.dtype), window_strides=(1,),
                                    padding=[(lo, kw - 1 - lo)], dimension_numbers=('NWC', 'WIO', 'NWC'),
                                    feature_group_count=c)


def _gated_delta_chunked(q, k, v, g, beta, s0):
    b, h, l, dk = k.shape
    dv = v.shape[-1]
    n = l // DN_CHUNK
    c = DN_CHUNK
    q, k, v = [t.reshape(b, h, n, c, -1) for t in (q, k, v)]
    beta = beta.reshape(b, h, n, c)
    g = jnp.cumsum(g.reshape(b, h, n, c), axis=-1)
    i = jnp.arange(c)
    causal = i[:, None] >= i[None, :]
    strict = i[:, None] > i[None, :]
    decay = jnp.where(causal, jnp.exp(jnp.where(causal, g[..., :, None] - g[..., None, :], 0.0)), 0.0)
    kb = k * beta[..., None]
    lmat = jnp.where(strict, jnp.einsum('bhnid,bhnjd->bhnij', kb, k) * decay, 0.0)
    a = lmat + jnp.eye(c, dtype=F32)
    u = lax.linalg.triangular_solve(a, v * beta[..., None], left_side=True, lower=True, unit_diagonal=True)
    w = lax.linalg.triangular_solve(a, kb * jnp.exp(g)[..., None], left_side=True, lower=True, unit_diagonal=True)
    attn = jnp.einsum('bhnid,bhnjd->bhnij', q, k) * decay
    q_dec = q * jnp.exp(g)[..., None]
    k_dec = k * jnp.exp(g[..., -1:] - g)[..., None]
    g_last = jnp.exp(g[..., -1])

    def step(state, xs):
        u_c, w_c, attn_c, q_c, k_c, gl_c = xs
        v_new = u_c - jnp.einsum('bhik,bhkv->bhiv', w_c, state)
        o_c = jnp.einsum('bhik,bhkv->bhiv', q_c, state) + jnp.einsum('bhij,bhjv->bhiv', attn_c, v_new)
        state = state * gl_c[..., None, None] + jnp.einsum('bhik,bhiv->bhkv', k_c, v_new)
        return state, o_c

    xs = tuple(jnp.moveaxis(t, 2, 0) for t in (u, w, attn, q_dec, k_dec, g_last))
    s_fin, o = lax.scan(step, s0.astype(F32), xs)
    return jnp.moveaxis(o, 0, 2).reshape(b, h, l, dv), s_fin


def _bidir_deltanet(qkv, z, b_raw, a_raw, conv_w, a_log, dt_bias, norm_g, s0_fwd, s0_bwd):
    b, l, _ = qkv.shape
    qkv = jax.nn.silu(_short_conv(qkv, conv_w))
    q, k, v = [_heads(t, DN_HEADS).astype(F32) for t in jnp.split(qkv, 3, axis=-1)]
    q = _l2norm(q) * DN_HEAD_DIM ** -0.5
    k = _l2norm(k)
    beta = jax.nn.sigmoid(b_raw.astype(F32)).reshape(b, l, 2, DN_HEADS).transpose(2, 0, 3, 1)
    a_in = a_raw.astype(F32).reshape(b, l, 2, DN_HEADS).transpose(2, 0, 3, 1)
    g = -jnp.exp(a_log.astype(F32))[:, None, :, None] * jax.nn.softplus(a_in + dt_bias.astype(F32)[:, None, :, None])
    o_f, s_f = _gated_delta_chunked(q, k, v, g[0], beta[0], s0_fwd)
    flip = lambda t: jnp.flip(t, axis=2)
    o_b, s_b = _gated_delta_chunked(flip(q), flip(k), flip(v), flip(g[1]), flip(beta[1]), s0_bwd)
    o = (o_f + flip(o_b)).transpose(0, 2, 1, 3)
    o = _rms_norm(o, norm_g) * jax.nn.silu(z.astype(F32).reshape(b, l, DN_HEADS, DN_HEAD_DIM))
    return o.reshape(b, l, DN_WIDTH), s_f, s_b


def _mixers(h, lp, ctx):
    b, l, _ = h.shape
    u_pool, q_na, k_na, v_na, qkv_dn, z_dn, b_dn, a_dn = _split_cols(h @ lp['w_in'], IN_SIZES)
    pooled = _multi_scale_pool(u_pool).reshape(b, l, N_POOL_GROUPS, POOL_GROUP)
    o_pool = jnp.einsum('blgc,gce->blge', pooled, lp['w_pool']).reshape(b, l, POOL_WIDTH) * lp['pool_scale']
    qn = _rms_norm(_heads(q_na, NA_HEADS), lp['q_norm'])
    kn = _rms_norm(_heads(k_na, NA_HEADS), lp['k_norm'])
    vn = _heads(v_na, NA_HEADS)
    if ctx is None:
        o_na = _context_attention(qn, kn, vn)
        s0 = jnp.zeros((b, 2, DN_HEADS, DN_HEAD_DIM, DN_HEAD_DIM), F32)
    else:
        o_na = _neighbourhood_attention(qn, kn, vn, ctx[0], ctx[1], lp['rpb'])
        s0 = ctx[2]
    o_na = o_na.transpose(0, 2, 1, 3).reshape(b, l, NA_WIDTH)
    o_dn, s_f, s_b = _bidir_deltanet(qkv_dn, z_dn, b_dn, a_dn, lp['conv'], lp['a_log'], lp['dt_bias'],
                                     lp['dn_norm'], s0[:, 0], s0[:, 1])
    mix = jnp.concatenate([o_pool, o_na.astype(h.dtype), o_dn.astype(h.dtype)], axis=-1) @ lp['w_out']
    new_ctx = (kn, vn, jnp.stack([s_f, s_b], axis=1)) if ctx is None else None
    return mix, new_ctx


def _moe(h, w_router, b_router, w_gate, w_up, w_down):
    b, l, d = h.shape
    t = h.reshape(-1, d)
    scores = jax.nn.sigmoid((t @ w_router).astype(F32))
    sel = scores + b_router.astype(F32)
    grp = jnp.sum(lax.top_k(sel.reshape(-1, N_EXPERT_GROUPS, EXPERTS_PER_GROUP), 2)[0], axis=-1)
    best = jnp.argmax(grp, axis=-1)
    in_group = (jnp.arange(N_EXPERTS) // EXPERTS_PER_GROUP)[None, :] == best[:, None]
    _, idx = lax.top_k(jnp.where(in_group, sel, -jnp.inf), TOP_K)
    w_sel = jnp.take_along_axis(scores, idx, axis=-1)
    w_sel = w_sel / jnp.sum(w_sel, axis=-1, keepdims=True)
    gates = jnp.sum(jax.nn.one_hot(idx, N_EXPERTS, dtype=F32) * w_sel[..., None], axis=1)
    hid = jax.nn.silu(jnp.einsum('td,edf->tef', t, w_gate)) * jnp.einsum('td,edf->tef', t, w_up)
    hid = hid * gates.astype(hid.dtype)[..., None]
    return jnp.einsum('tef,efd->td', hid, w_down).reshape(b, l, d)


def _layer(x, cond, lp, w_router, b_router, ctx):
    mod = (jax.nn.silu(cond) @ lp['w_ada'] + lp['b_ada'])[:, None, :]
    sh1, sc1, g1, sh2, sc2, g2 = jnp.split(mod, 6, axis=-1)
    h = _rms_norm(x, lp['norm1']) * (1 + sc1) + sh1
    mix, new_ctx = _mixers(h, lp, ctx)
    x = x + g1 * mix
    h = _rms_norm(x, lp['norm2']) * (1 + sc2) + sh2
    x = x + g2 * _moe(h, w_router, b_router, lp['w_gate'], lp['w_up'], lp['w_down'])
    return x, new_ctx


def setup_inputs(seed: int = 0) -> dict:
    key = jax.random.key(seed)
    ks = jax.random.split(key, 32)
    nrm = lambda k, shape, s: jax.random.normal(k, shape, F32) * s
    d = D_MODEL
    dt = jnp.exp(jax.random.uniform(ks[20], (DEPTH, 2, DN_HEADS), F32, math.log(1e-3), math.log(1e-1)))
    return {
        'x_prompt': nrm(ks[0], (BATCH, SEQ, d), 1.0),
        'x_sample': nrm(ks[1], (DEC_BATCH, DEC_SEQ, d), 1.0),
        'cache_na_k': nrm(ks[2], (DEC_BATCH, DEPTH, NA_HEADS, PAST_LEN, NA_HEAD_DIM), 1.0),
        'cache_na_v': nrm(ks[3], (DEC_BATCH, DEPTH, NA_HEADS, PAST_LEN, NA_HEAD_DIM), 1.0),
        'state_dn': nrm(ks[4], (DEC_BATCH, DEPTH, 2, DN_HEADS, DN_HEAD_DIM, DN_HEAD_DIM), DN_HEAD_DIM ** -0.5),
        'c': nrm(ks[5], (DEC_BATCH, d), 1.0),
        'c_ctx': nrm(ks[6], (d,), 1.0),
        'norm1_g': 1.0 + nrm(ks[7], (DEPTH, d), 0.1),
        'norm2_g': 1.0 + nrm(ks[8], (DEPTH, d), 0.1),
        'w_ada': nrm(ks[9], (DEPTH, d, 6 * d), 0.5 * d ** -0.5),
        'b_ada': nrm(ks[10], (DEPTH, 6 * d), 0.01),
        'w_in': nrm(ks[11], (DEPTH, d, IN_WIDTH), d ** -0.5),
        'w_out': nrm(ks[12], (DEPTH, MIX_WIDTH, d), MIX_WIDTH ** -0.5),
        'w_pool': nrm(ks[13], (DEPTH, N_POOL_GROUPS, POOL_GROUP, POOL_GROUP), POOL_GROUP ** -0.5),
        'pool_scale': 1.0 + nrm(ks[14], (DEPTH, POOL_WIDTH), 0.1),
        'q_norm_g': 1.0 + nrm(ks[15], (DEPTH, NA_HEAD_DIM), 0.1),
        'k_norm_g': 1.0 + nrm(ks[16], (DEPTH, NA_HEAD_DIM), 0.1),
        'rpb': nrm(ks[17], (DEPTH, NA_HEADS, 2 * NA_ROWS - 1, 2 * NA_COLS - 1), 0.1),
        'dn_conv_w': nrm(ks[18], (DEPTH, DN_CONV, 3 * DN_WIDTH), DN_CONV ** -0.5),
        'dn_a_log': jnp.log(jax.random.uniform(ks[19], (DEPTH, 2, DN_HEADS), F32, 1.0, 16.0)),
        'dn_dt_bias': dt + jnp.log(-jnp.expm1(-dt)),
        'dn_norm_g': 1.0 + nrm(ks[21], (DEPTH, DN_HEAD_DIM), 0.1),
        'w_router': nrm(ks[22], (d, N_EXPERTS), d ** -0.5),
        'b_router': nrm(ks[23], (N_EXPERTS,), 0.01),
        'w_gate': nrm(ks[24], (DEPTH, N_EXPERTS, d, D_EXPERT), d ** -0.5),
        'w_up': nrm(ks[25], (DEPTH, N_EXPERTS, d, D_EXPERT), d ** -0.5),
        'w_down': nrm(ks[26], (DEPTH, N_EXPERTS, D_EXPERT, d), D_EXPERT ** -0.5),
    }


def reference(x_prompt, x_sample, cache_na_k, cache_na_v, state_dn, c, c_ctx, norm1_g, norm2_g, w_ada, b_ada,
              w_in, w_out, w_pool, pool_scale, q_norm_g, k_norm_g, rpb, dn_conv_w, dn_a_log, dn_dt_bias, dn_norm_g,
              w_router, b_router, w_gate, w_up, w_down):
    y_prompt = x_prompt
    y_sample = x_sample
    new_k, new_v, new_s = [], [], []
    for l in range(DEPTH):
        lp = {'norm1': norm1_g[l], 'norm2': norm2_g[l], 'w_ada': w_ada[l], 'b_ada': b_ada[l],
              'w_in': w_in[l], 'w_out': w_out[l], 'w_pool': w_pool[l], 'pool_scale': pool_scale[l],
              'q_norm': q_norm_g[l], 'k_norm': k_norm_g[l], 'rpb': rpb[l], 'conv': dn_conv_w[l],
              'a_log': dn_a_log[l], 'dt_bias': dn_dt_bias[l], 'dn_norm': dn_norm_g[l],
              'w_gate': w_gate[l], 'w_up': w_up[l], 'w_down': w_down[l]}
        y_prompt, (k_l, v_l, s_l) = _layer(y_prompt, c_ctx[None, :], lp, w_router, b_router, None)
        new_k.append(k_l)
        new_v.append(v_l)
        new_s.append(s_l)
        y_sample, _ = _layer(y_sample, c, lp, w_router, b_router,
                             (cache_na_k[:, l], cache_na_v[:, l], state_dn[:, l]))
    new_cache_na_k = jnp.stack(new_k, axis=1)
    new_cache_na_v = jnp.stack(new_v, axis=1)
    new_state_dn = jnp.stack(new_s, axis=1)
    return (y_prompt, y_sample, new_cache_na_k, new_cache_na_v, new_state_dn)
```

```python
import functools
import math

import numpy as np
import jax
import jax.numpy as jnp
from jax import lax
from jax.experimental import pallas as pl
from jax.experimental.pallas import tpu as pltpu

F32 = jnp.float32
BF16 = jnp.bfloat16

D_MODEL = 1024
GRID_W = 64
POOL_WIDTH = 256
POOL_WINDOWS = (2, 4, 8, 16)
POOL_GROUP = 64
NA_WIDTH = 512
HEAD_DIM = 64
NA_HEADS = 8
NA_ROWS = 8
NA_COLS = 16
DN_WIDTH = 256
DN_HEADS = 4
DN_CONV = 4
DN_CHUNK = 64
N_EXPERTS = 16
N_EXPERT_GROUPS = 4
EXPERTS_PER_GROUP = 4
D_EXPERT = 512
EPS = 1e-6
NEG = -1e30

COL_POOL = 0
COL_Q = 256
COL_K = 768
COL_V = 1280
COL_DN = 1792
COL_Z = 2560
COL_BA = 2816
IN_WIDTH = 2832
IN_PAD = 2944

NA_QROWS = 4
NA_KROWS = NA_ROWS + NA_QROWS

VMEM_LIMIT = 56 * 1024 * 1024
TOKEN_TILE = 512
MOE_TILE = 1024


def _cparams(*sem):
    return pltpu.CompilerParams(dimension_semantics=sem, vmem_limit_bytes=VMEM_LIMIT)


def _dot(a, b):
    return jnp.dot(a, b, preferred_element_type=F32)


def _dot_nt(a, b):
    return lax.dot_general(a, b, (((1,), (1,)), ((), ())), preferred_element_type=F32)


def _dot_tn(a, b):
    return lax.dot_general(a, b, (((0,), (0,)), ((), ())), preferred_element_type=F32)


def _split2(a):
    hi = a.astype(BF16)
    lo = (a - hi.astype(F32)).astype(BF16)
    return hi, lo


def _split3(a):
    hi = a.astype(BF16)
    r = a - hi.astype(F32)
    mid = r.astype(BF16)
    lo = (r - mid.astype(F32)).astype(BF16)
    return hi, mid, lo


def _mm(a, b, passes, dotf=_dot):
    if passes == 1:
        return dotf(a.astype(BF16), b.astype(BF16))
    ah, al = _split2(a)
    bh, bl = _split2(b)
    return dotf(ah, bh) + (dotf(ah, bl) + dotf(al, bh))


def _mm_exact_b(a, b01, dotf=_dot):
    a0, a1, a2 = _split3(a)
    return dotf(a0, b01) + (dotf(a1, b01) + dotf(a2, b01))


def _mm_exact_a(a01, b, dotf=_dot):
    b0, b1, b2 = _split3(b)
    return dotf(a01, b0) + (dotf(a01, b1) + dotf(a01, b2))


def _iota(shape, dim):
    return lax.broadcasted_iota(jnp.int32, shape, dim)


def _head_blockdiag(rows, cols):
    return jnp.where(_iota((rows, cols), 0) // HEAD_DIM == _iota((rows, cols), 1) // HEAD_DIM, 1.0, 0.0).astype(BF16)


def _silu(x):
    return x * jax.nn.sigmoid(x)


def _ada_kernel(c_ref, w_ref, b_ref, o_ref):
    o_ref[0] = _mm(_silu(c_ref[...]), w_ref[0], 3) + b_ref[0]


def _ada_mod(conds, w_ada, b_ada):
    depth, d, n = w_ada.shape
    tn = 1536
    return pl.pallas_call(
        _ada_kernel,
        grid=(depth, n // tn),
        in_specs=[pl.BlockSpec((8, d), lambda l, j: (0, 0)),
                  pl.BlockSpec((1, d, tn), lambda l, j: (l, 0, j)),
                  pl.BlockSpec((1, 1, tn), lambda l, j: (l, 0, j))],
        out_specs=pl.BlockSpec((1, 8, tn), lambda l, j: (l, 0, j)),
        out_shape=jax.ShapeDtypeStruct((depth, 8, n), F32),
        compiler_params=_cparams("parallel", "parallel"),
        name="ada_mod",
    )(conds, w_ada, b_ada.reshape(depth, 1, n))


def _inproj_kernel(x_ref, g_ref, sc_ref, sh_ref, w_ref, qg_ref, kg_ref, o_ref):
    x = x_ref[0]
    h = x * lax.rsqrt(jnp.mean(x * x, axis=-1, keepdims=True) + EPS) * g_ref[...]
    h = (h * (1.0 + sc_ref[0]) + sh_ref[0]).astype(BF16)
    bd = _head_blockdiag(256, 256)
    for c0 in range(0, IN_PAD, 256):
        cw = min(256, IN_PAD - c0)
        r = _dot(h, w_ref[:, c0:c0 + cw])
        if COL_Q <= c0 < COL_V:
            gain = qg_ref[...] if c0 < COL_K else kg_ref[...]
            r = r * lax.rsqrt(_mm_exact_b(r * r, bd) * (1.0 / HEAD_DIM) + EPS) * gain
        o_ref[0, :, c0:c0 + cw] = r


def _in_proj(x, mod_sc, mod_sh, norm_g, w_pad, qg, kg, tm):
    b, l, d = x.shape
    bc = mod_sc.shape[0]
    mod_idx = (lambda i, j: (i, 0, 0)) if bc == b else (lambda i, j: (0, 0, 0))
    return pl.pallas_call(
        _inproj_kernel,
        grid=(b, l // tm),
        in_specs=[pl.BlockSpec((1, tm, d), lambda i, j: (i, j, 0)),
                  pl.BlockSpec((1, d), lambda i, j: (0, 0)),
                  pl.BlockSpec((1, 1, d), mod_idx),
                  pl.BlockSpec((1, 1, d), mod_idx),
                  pl.BlockSpec((d, IN_PAD), lambda i, j: (0, 0)),
                  pl.BlockSpec((1, 256), lambda i, j: (0, 0)),
                  pl.BlockSpec((1, 256), lambda i, j: (0, 0))],
        out_specs=pl.BlockSpec((1, tm, IN_PAD), lambda i, j: (i, j, 0)),
        out_shape=jax.ShapeDtypeStruct((b, l, IN_PAD), F32),
        compiler_params=_cparams("parallel", "parallel"),
        name="in_proj",
    )(x, norm_g, mod_sc, mod_sh, w_pad, qg, kg)


def _pool_kernel(u_ref, w_ref, s_ref, o_ref, p1, p2, p4, p8, p16):
    l = u_ref.shape[1]
    p1[0:16] = jnp.zeros((16, POOL_WIDTH), F32)
    p1[16 + l:48 + l] = jnp.zeros((32, POOL_WIDTH), F32)
    u = u_ref[0]
    p1[16:16 + l] = u
    p2[0:l + 40] = p1[0:l + 40] + p1[1:l + 41]
    p4[0:l + 32] = p2[0:l + 32] + p2[2:l + 34]
    p8[0:l + 24] = p4[0:l + 24] + p4[4:l + 28]
    p16[0:l + 16] = p8[0:l + 16] + p8[8:l + 24]
    grp = _iota((l, POOL_WIDTH), 1) // POOL_GROUP
    t = _iota((l, POOL_WIDTH), 0)
    half = jnp.where(grp == 0, 1, jnp.where(grp == 1, 2, jnp.where(grp == 2, 4, 8)))
    cnt = (jnp.minimum(t + half, l) - jnp.maximum(t - half, 0)).astype(F32)
    wsum = jnp.where(grp == 0, p2[15:15 + l],
                     jnp.where(grp == 1, p4[14:14 + l],
                               jnp.where(grp == 2, p8[12:12 + l], p16[8:8 + l])))
    pooled = wsum / cnt - u
    o_ref[0] = _dot(pooled.astype(BF16), w_ref[...]) * s_ref[...]


def _pool_mixer(proj, w_bd, scale):
    b, l, _ = proj.shape
    pad = pltpu.VMEM((l + 48, POOL_WIDTH), F32)
    return pl.pallas_call(
        _pool_kernel,
        grid=(b,),
        in_specs=[pl.BlockSpec((1, l, POOL_WIDTH), lambda i: (i, 0, COL_POOL // POOL_WIDTH)),
                  pl.BlockSpec((POOL_WIDTH, POOL_WIDTH), lambda i: (0, 0)),
                  pl.BlockSpec((1, POOL_WIDTH), lambda i: (0, 0))],
        out_specs=pl.BlockSpec((1, l, POOL_WIDTH), lambda i: (i, 0, 0)),
        out_shape=jax.ShapeDtypeStruct((b, l, POOL_WIDTH), F32),
        scratch_shapes=[pad, pad, pad, pad, pad],
        compiler_params=_cparams("parallel"),
        name="pool_mixer",
    )(proj, w_bd, scale)


def _softmax_pv(scores, values):
    m = functools.reduce(jnp.maximum, [jnp.max(s, axis=-1, keepdims=True) for s in scores])
    ps = [jnp.exp(s - m) for s in scores]
    den = functools.reduce(lambda a, c: a + c, [jnp.sum(p, axis=-1, keepdims=True) for p in ps])
    acc = functools.reduce(lambda a, c: a + c, [_dot(p.astype(BF16), v) for p, v in zip(ps, values)])
    return acc / den


def _ctx_attn_kernel(q_ref, k_ref, v_ref, o_ref):
    q = q_ref[0] * (HEAD_DIM ** -0.5)
    kb = k_ref[0].astype(BF16)
    vb = v_ref[0].astype(BF16)
    lane = _iota(q.shape, 1)
    outs = []
    for h in range(2):
        qm = jnp.where(lane // HEAD_DIM == h, q, 0.0).astype(BF16)
        outs.append(_softmax_pv([_dot_nt(qm, kb)], [vb]))
    o_ref[0] = jnp.where(lane < HEAD_DIM, outs[0], outs[1])


def _context_attention(proj):
    b, s, _ = proj.shape
    npair = NA_HEADS // 2
    blk = lambda col: pl.BlockSpec((1, s, 128), lambda i, p: (i, 0, col // 128 + p))
    return pl.pallas_call(
        _ctx_attn_kernel,
        grid=(b, npair),
        in_specs=[blk(COL_Q), blk(COL_K), blk(COL_V)],
        out_specs=pl.BlockSpec((1, s, 128), lambda i, p: (i, 0, p)),
        out_shape=jax.ShapeDtypeStruct((b, s, NA_WIDTH), F32),
        compiler_params=_cparams("parallel", "parallel"),
        name="context_attention",
    )(proj, proj, proj)


def _na_attn_kernel(q_ref, k_ref, v_ref, kc_ref, vc_ref, bias_ref, o_ref):
    l = q_ref.shape[1]
    rows = l // GRID_W
    ngroups = rows // NA_QROWS
    nq = NA_QROWS * GRID_W
    nk = NA_KROWS * GRID_W
    kc = kc_ref[0].astype(BF16)
    vc = vc_ref[0].astype(BF16)
    lane = _iota((nq, 128), 1)

    def group(g, carry):
        q0 = pl.multiple_of(g * nq, nq)
        k0 = pl.multiple_of(jnp.clip(NA_QROWS * g - NA_ROWS // 2, 0, rows - NA_KROWS) * GRID_W, GRID_W)
        pat = jnp.where(g == 0, 0, jnp.where(g == ngroups - 1, 2, 1))
        q = q_ref[0, pl.ds(q0, nq), :] * (HEAD_DIM ** -0.5)
        kl = k_ref[0, pl.ds(k0, nk), :].astype(BF16)
        vl = v_ref[0, pl.ds(k0, nk), :].astype(BF16)
        outs = []
        for h in range(2):
            qm = jnp.where(lane // HEAD_DIM == h, q, 0.0).astype(BF16)
            s_loc = _dot_nt(qm, kl) + bias_ref[h, pat]
            s_ctx = _dot_nt(qm, kc)
            outs.append(_softmax_pv([s_loc, s_ctx], [vl, vc]))
        o_ref[0, pl.ds(q0, nq), :] = jnp.where(lane < HEAD_DIM, outs[0], outs[1])
        return carry

    lax.fori_loop(0, ngroups, group, 0)


def _na_bias(rpb, rows):
    assert rows >= 2 * NA_KROWS - NA_ROWS and rows % NA_QROWS == 0
    ngroups = rows // NA_QROWS
    col = np.arange(GRID_W)
    cs = np.clip(col - NA_COLS // 2, 0, GRID_W - NA_COLS)
    in_win = (col[None, :] >= cs[:, None]) & (col[None, :] < cs[:, None] + NA_COLS)
    dc = np.clip(col[None, :] - col[:, None] + NA_COLS - 1, 0, 2 * NA_COLS - 2)
    dr_all, ok_all = [], []
    for g in (0, 1, ngroups - 1):
        u = int(np.clip(NA_QROWS * g - NA_ROWS // 2, 0, rows - NA_KROWS))
        r = NA_QROWS * g + np.arange(NA_QROWS)
        r0 = np.clip(r - NA_ROWS // 2, 0, rows - NA_ROWS)
        kr = u + np.arange(NA_KROWS)
        row_ok = (kr[None, :] >= r0[:, None]) & (kr[None, :] < r0[:, None] + NA_ROWS)
        dr = np.clip(kr[None, :] - r[:, None] + NA_ROWS - 1, 0, 2 * NA_ROWS - 2)
        ok_all.append(row_ok[:, None, :, None] & in_win[None, :, None, :])
        dr_all.append(np.broadcast_to(dr[:, None, :, None], ok_all[-1].shape))
    ok = np.stack(ok_all).reshape(3, NA_QROWS * GRID_W, NA_KROWS * GRID_W)
    dr = np.stack(dr_all).reshape(3, NA_QROWS * GRID_W, NA_KROWS * GRID_W)
    dcb = np.broadcast_to(dc[None, None, :, None, :], (3, NA_QROWS, GRID_W, NA_KROWS, GRID_W)).reshape(dr.shape)
    vals = rpb.astype(F32)[:, dr, dcb]
    return jnp.where(ok[None], vals, NEG)


def _neighbourhood_attention(proj, k_ctx, v_ctx, bias):
    b, l, _ = proj.shape
    p = k_ctx.shape[1]
    npair = NA_HEADS // 2
    nq, nk = bias.shape[2], bias.shape[3]
    blk = lambda col: pl.BlockSpec((1, l, 128), lambda pr, i: (i, 0, col // 128 + pr))
    ctx = pl.BlockSpec((1, p, 128), lambda pr, i: (i, 0, pr))
    return pl.pallas_call(
        _na_attn_kernel,
        grid=(npair, b),
        in_specs=[blk(COL_Q), blk(COL_K), blk(COL_V), ctx, ctx,
                  pl.BlockSpec((2, 3, nq, nk), lambda pr, i: (pr, 0, 0, 0))],
        out_specs=pl.BlockSpec((1, l, 128), lambda pr, i: (i, 0, pr)),
        out_shape=jax.ShapeDtypeStruct((b, l, NA_WIDTH), F32),
        compiler_params=_cparams("parallel", "parallel"),
        name="neighbourhood_attention",
    )(proj, proj, proj, k_ctx, v_ctx, bias)


DN_INV_PASSES = 3
DN_PRE_TILE = 256


def _tile4(x):
    return jnp.concatenate([x, x, x, x], axis=0)


def _dn_consts(d):
    c, w = DN_CHUNK, DN_WIDTH
    pos = _iota((c, w), 1) % c
    row = _iota((c, w), 0)
    i64 = _iota((c, c), 0)
    j64 = _iota((c, c), 1)
    one = lambda m: jnp.where(m, 1.0, 0.0).astype(BF16)
    before = (lambda a, b: a <= b) if d == 0 else (lambda a, b: a >= b)
    sel_r = _iota((128, w), 0)
    sel_h = _iota((128, w), 1) // HEAD_DIM
    return dict(
        incl=before(pos, row),
        strict=before(pos, row) & (pos != row),
        gmask=before(row, pos),
        eye=jnp.where(pos == row, 1.0, 0.0),
        cum=jnp.concatenate([one(before(j64, i64)), jnp.ones((c, c), BF16)], axis=0),
        ones=jnp.ones((c, c), BF16),
        e_beta=one(sel_r == DN_HEADS * d + sel_h),
        e_g=one(sel_r == 2 * DN_HEADS + DN_HEADS * d + sel_h))


def _dn_kernel(qr_ref, kr_ref, vr_ref, z_ref, ba_ref, cw_ref, alog_ref, dtb_ref, ng_ref, s0_ref,
               o_ref, sfin_ref, xpad, qs, ks, vs, od, sbd):
    l = qr_ref.shape[1]
    nc = l // DN_CHUNK
    c = DN_CHUNK
    w = DN_WIDTH
    tr = min(l, DN_PRE_TILE)

    for i, (src, dst) in enumerate(((qr_ref, qs), (kr_ref, ks), (vr_ref, vs))):
        xpad[0:8] = jnp.zeros((8, w), F32)
        xpad[8 + l:16 + l] = jnp.zeros((8, w), F32)
        xpad[8:8 + l] = src[0]
        bd = _head_blockdiag(w, w)
        for r0 in range(0, l, tr):
            acc = cw_ref[0:1, i * w:(i + 1) * w] * xpad[r0 + 7:r0 + 7 + tr]
            for j in range(1, DN_CONV):
                acc = acc + cw_ref[j:j + 1, i * w:(i + 1) * w] * xpad[r0 + 7 + j:r0 + 7 + j + tr]
            y = _silu(acc)
            if i == 0:
                y = y * lax.rsqrt(_mm_exact_b(y * y, bd) + EPS) * (HEAD_DIM ** -0.5)
            elif i == 1:
                y = y * lax.rsqrt(_mm_exact_b(y * y, bd) + EPS)
            dst[r0:r0 + tr] = y

    sbd[0] = s0_ref[0, 0]
    sbd[1] = s0_ref[0, 1]

    def chunk_step(c0, d):
        cst = _dn_consts(d)
        bdm = _head_blockdiag(w, w) > 0
        bdiag = lambda x: jnp.where(bdm, _tile4(x), 0.0)
        qc = qs[pl.ds(c0, c)]
        kc = ks[pl.ds(c0, c)]
        vc = vs[pl.ds(c0, c)]
        ba = ba_ref[0, pl.ds(c0, c), :]
        beta = _mm_exact_b(jax.nn.sigmoid(ba), cst["e_beta"])
        x = ba + dtb_ref[...]
        softplus = jnp.maximum(x, 0.0) + jnp.log1p(jnp.exp(-jnp.abs(x)))
        g = _mm_exact_b(-jnp.exp(alog_ref[...]) * softplus, cst["e_g"])
        gg = _mm_exact_a(cst["cum"], g)
        gcum, gtot = gg[0:c], gg[c:2 * c]
        grow = _mm_exact_a(cst["ones"], jnp.where(cst["gmask"], g, 0.0))
        incl = cst["incl"]
        decay = jnp.where(incl, jnp.exp(jnp.where(incl, gcum - grow, 0.0)), 0.0)
        eg = jnp.exp(gcum)
        kb = kc * beta
        kq = _dot_nt(jnp.concatenate([kb, qc], axis=0).astype(BF16), bdiag(kc).astype(BF16))
        nm = jnp.where(cst["strict"], kq[0:c] * decay, 0.0)
        attn = kq[c:2 * c] * decay
        t = cst["eye"] - nm
        p = _mm(nm, bdiag(nm), DN_INV_PASSES)
        for _ in range(4):
            tp = _mm(jnp.concatenate([t, p], axis=0), bdiag(p), DN_INV_PASSES)
            t = t + tp[0:c]
            p = tp[c:2 * c]
        t = t + _mm(t, bdiag(p), DN_INV_PASSES)
        u = _mm(t, bdiag(vc * beta), DN_INV_PASSES)
        wmat = _mm(t, bdiag(kb * eg), DN_INV_PASSES)
        s = sbd[d]
        ws = _mm(jnp.concatenate([wmat, qc * eg], axis=0), s, 1)
        vnew = u - ws[0:c]
        o = ws[c:2 * c] + _mm(attn, bdiag(vnew), 1)
        kdec = kc * jnp.exp(gtot - gcum)
        upd = _dot_tn(kdec.astype(BF16), vnew.astype(BF16))
        sbd[d] = s * jnp.exp(gtot[0:1]) + jnp.where(bdm, upd, 0.0)
        od[d, pl.ds(c0, c)] = o

    def body(n, carry):
        chunk_step(pl.multiple_of(n * c, c), 0)
        chunk_step(pl.multiple_of((nc - 1 - n) * c, c), 1)
        return carry

    lax.fori_loop(0, nc, body, 0)
    sfin_ref[0, 0] = sbd[0]
    sfin_ref[0, 1] = sbd[1]

    bd = _head_blockdiag(w, w)
    for r0 in range(0, l, tr):
        o = od[0, r0:r0 + tr] + od[1, r0:r0 + tr]
        o = o * lax.rsqrt(_mm_exact_b(o * o, bd) * (1.0 / HEAD_DIM) + EPS) * ng_ref[...]
        o_ref[0, r0:r0 + tr] = o * _silu(z_ref[0, r0:r0 + tr])


def _deltanet(proj, conv_w, alog_row, dtb_row, norm_row, s0_bd):
    b, l, _ = proj.shape
    w = DN_WIDTH
    col = lambda c0: pl.BlockSpec((1, l, w), lambda i: (i, 0, c0 // w))
    return pl.pallas_call(
        _dn_kernel,
        grid=(b,),
        in_specs=[col(COL_DN), col(COL_DN + w), col(COL_DN + 2 * w), col(COL_Z),
                  pl.BlockSpec((1, l, 128), lambda i: (i, 0, COL_BA // 128)),
                  pl.BlockSpec((DN_CONV, 3 * w), lambda i: (0, 0)),
                  pl.BlockSpec((1, 128), lambda i: (0, 0)),
                  pl.BlockSpec((1, 128), lambda i: (0, 0)),
                  pl.BlockSpec((1, w), lambda i: (0, 0)),
                  pl.BlockSpec((1, 2, w, w), lambda i: (i, 0, 0, 0))],
        out_specs=[pl.BlockSpec((1, l, w), lambda i: (i, 0, 0)),
                   pl.BlockSpec((1, 2, w, w), lambda i: (i, 0, 0, 0))],
        out_shape=[jax.ShapeDtypeStruct((b, l, w), F32),
                   jax.ShapeDtypeStruct((b, 2, w, w), F32)],
        scratch_shapes=[pltpu.VMEM((l + 16, w), F32),
                        pltpu.VMEM((l, w), F32), pltpu.VMEM((l, w), F32), pltpu.VMEM((l, w), F32),
                        pltpu.VMEM((2, l, w), F32),
                        pltpu.VMEM((2, w, w), F32)],
        compiler_params=_cparams("parallel"),
        name="deltanet",
    )(proj, proj, proj, proj, proj, conv_w, alog_row, dtb_row, norm_row, s0_bd)


def _top2_sum(a, b, c, d):
    hi1, lo1 = jnp.maximum(a, b), jnp.minimum(a, b)
    hi2, lo2 = jnp.maximum(c, d), jnp.minimum(c, d)
    return jnp.maximum(hi1, hi2) + jnp.maximum(jnp.minimum(hi1, hi2), jnp.maximum(lo1, lo2))


def _outproj_kernel(op_ref, on_ref, od_ref, x_ref, w_ref, g1_ref, n2_ref, sc_ref, sh_ref, wr_ref, br_ref,
                    x1_ref, h2_ref, gt_ref):
    mix = (_dot(op_ref[0].astype(BF16), w_ref[0:POOL_WIDTH])
           + _dot(on_ref[0].astype(BF16), w_ref[POOL_WIDTH:POOL_WIDTH + NA_WIDTH])
           + _dot(od_ref[0].astype(BF16), w_ref[POOL_WIDTH + NA_WIDTH:D_MODEL]))
    x1 = x_ref[0] + g1_ref[0] * mix
    x1_ref[0] = x1
    h2 = x1 * lax.rsqrt(jnp.mean(x1 * x1, axis=-1, keepdims=True) + EPS) * n2_ref[...]
    h2 = h2 * (1.0 + sc_ref[0]) + sh_ref[0]
    h2_ref[0] = h2.astype(BF16)
    scores = jax.nn.sigmoid(_mm(wr_ref[...], h2, 3, _dot_nt))
    sel = scores + br_ref[...]
    rows = [sel[e:e + 1] for e in range(N_EXPERTS)]
    gsum = [_top2_sum(*rows[EXPERTS_PER_GROUP * g:EXPERTS_PER_GROUP * (g + 1)]) for g in range(N_EXPERT_GROUPS)]
    best = jnp.zeros_like(gsum[0], dtype=jnp.int32)
    bestv = gsum[0]
    for g in range(1, N_EXPERT_GROUPS):
        better = gsum[g] > bestv
        best = jnp.where(better, g, best)
        bestv = jnp.where(better, gsum[g], bestv)
    picked = []
    for e in range(N_EXPERTS):
        g = e // EXPERTS_PER_GROUP
        rank = jnp.zeros_like(best)
        for o in range(EXPERTS_PER_GROUP * g, EXPERTS_PER_GROUP * (g + 1)):
            if o != e:
                ahead = (rows[o] > rows[e]) if o > e else (rows[o] >= rows[e])
                rank = rank + ahead.astype(jnp.int32)
        picked.append(jnp.where((best == g) & (rank < 2), scores[e:e + 1], 0.0))
    den = functools.reduce(lambda a, c: a + c, picked)
    for e in range(N_EXPERTS):
        gt_ref[0, e:e + 1, :] = picked[e] / den


def _out_proj(o_pool, o_na, o_dn, x, w_out, g1, norm_g, sc2, sh2, w_router_t, b_router, tm):
    b, l, d = x.shape
    bc = g1.shape[0]
    mod_idx = (lambda i, j: (i, 0, 0)) if bc == b else (lambda i, j: (0, 0, 0))
    tok = lambda width: pl.BlockSpec((1, tm, width), lambda i, j: (i, j, 0))
    full = lambda a: pl.BlockSpec(a.shape, lambda i, j: tuple(0 for _ in a.shape))
    mod = pl.BlockSpec((1, 1, d), mod_idx)
    return pl.pallas_call(
        _outproj_kernel,
        grid=(b, l // tm),
        in_specs=[tok(POOL_WIDTH), tok(NA_WIDTH), tok(DN_WIDTH), tok(d), full(w_out), mod, full(norm_g), mod, mod,
                  full(w_router_t), full(b_router)],
        out_specs=[tok(d), tok(d), pl.BlockSpec((1, N_EXPERTS, tm), lambda i, j: (i, 0, j))],
        out_shape=[jax.ShapeDtypeStruct((b, l, d), F32),
                   jax.ShapeDtypeStruct((b, l, d), BF16),
                   jax.ShapeDtypeStruct((b, N_EXPERTS, l), F32)],
        compiler_params=_cparams("parallel", "parallel"),
        name="out_proj",
    )(o_pool, o_na, o_dn, x, w_out, g1, norm_g, sc2, sh2, w_router_t, b_router)


def _moe_kernel(h_ref, gates_ref, wg_ref, wu_ref, wd_ref, x1_ref, g2_ref, o_ref, acc_ref):
    e = pl.program_id(2)

    @pl.when(e == 0)
    def _():
        acc_ref[...] = jnp.zeros_like(acc_ref)

    h = h_ref[0]
    gates = gates_ref[0]
    gate = jnp.sum(jnp.where(_iota(gates.shape, 1) == e, gates, 0.0), axis=-1, keepdims=True)
    hid = _silu(_dot(h, wg_ref[0])) * _dot(h, wu_ref[0]) * gate
    acc_ref[...] += _dot(hid.astype(BF16), wd_ref[0])

    @pl.when(e == N_EXPERTS - 1)
    def _():
        o_ref[0] = x1_ref[0] + g2_ref[0] * acc_ref[...]


def _moe(h2, gates, w_gate, w_up, w_down, x1, g2, tm):
    b, l, d = x1.shape
    bc = g2.shape[0]
    mod_idx = (lambda i, j, e: (i, 0, 0)) if bc == b else (lambda i, j, e: (0, 0, 0))
    tok = lambda width: pl.BlockSpec((1, tm, width), lambda i, j, e: (i, j, 0))
    return pl.pallas_call(
        _moe_kernel,
        grid=(b, l // tm, N_EXPERTS),
        in_specs=[tok(d), tok(N_EXPERTS),
                  pl.BlockSpec((1, d, D_EXPERT), lambda i, j, e: (e, 0, 0)),
                  pl.BlockSpec((1, d, D_EXPERT), lambda i, j, e: (e, 0, 0)),
                  pl.BlockSpec((1, D_EXPERT, d), lambda i, j, e: (e, 0, 0)),
                  tok(d), pl.BlockSpec((1, 1, d), mod_idx)],
        out_specs=tok(d),
        out_shape=jax.ShapeDtypeStruct((b, l, d), F32),
        scratch_shapes=[pltpu.VMEM((tm, d), F32)],
        compiler_params=_cparams("parallel", "parallel", "arbitrary"),
        name="moe",
    )(h2, gates, w_gate, w_up, w_down, x1, g2)


def _layer_params(l, norm1_g, norm2_g, w_in, w_out, w_pool, pool_scale, q_norm_g, k_norm_g, dn_conv_w, dn_a_log,
                  dn_dt_bias, dn_norm_g, w_gate, w_up, w_down):
    d = D_MODEL
    row128 = lambda v: jnp.zeros((1, 128), F32).at[0, 2 * DN_HEADS:4 * DN_HEADS].set(v.reshape(-1).astype(F32))
    w_bd = jnp.zeros((POOL_WIDTH, POOL_WIDTH), F32)
    for g in range(len(POOL_WINDOWS)):
        sl = slice(g * POOL_GROUP, (g + 1) * POOL_GROUP)
        w_bd = w_bd.at[sl, sl].set(w_pool[l, g])
    return dict(
        norm1=norm1_g[l].reshape(1, d), norm2=norm2_g[l].reshape(1, d),
        w_in=jnp.pad(w_in[l], ((0, 0), (0, IN_PAD - IN_WIDTH))).astype(BF16),
        w_out=w_out[l].astype(BF16),
        w_pool=w_bd.astype(BF16), pool_scale=pool_scale[l].reshape(1, POOL_WIDTH),
        qg=jnp.tile(q_norm_g[l], 256 // HEAD_DIM).reshape(1, 256),
        kg=jnp.tile(k_norm_g[l], 256 // HEAD_DIM).reshape(1, 256),
        conv=dn_conv_w[l], alog=row128(dn_a_log[l]), dtb=row128(dn_dt_bias[l]),
        dn_norm=jnp.tile(dn_norm_g[l], DN_HEADS).reshape(1, DN_WIDTH),
        w_gate=w_gate[l].astype(BF16), w_up=w_up[l].astype(BF16), w_down=w_down[l].astype(BF16))


def _state_to_blockdiag(s):
    b = s.shape[0]
    eye = jnp.eye(DN_HEADS, dtype=s.dtype)
    return (s[:, :, :, :, None, :] * eye[None, None, :, None, :, None]).reshape(b, 2, DN_WIDTH, DN_WIDTH)


def _blockdiag_to_state(s):
    b = s.shape[0]
    s6 = s.reshape(b, 2, DN_HEADS, HEAD_DIM, DN_HEADS, HEAD_DIM)
    return jnp.stack([s6[:, :, h, :, h, :] for h in range(DN_HEADS)], axis=2)


def _layer(x, mod, lp, w_router_t, b_router, ctx):
    b, l, d = x.shape
    sh1, sc1, g1, sh2, sc2, g2 = [mod[:, i] for i in range(6)]
    fold = (lambda t: t.reshape(1, b * l, t.shape[-1])) if mod.shape[0] == 1 else (lambda t: t)
    unfold = lambda t: t.reshape(b, l, t.shape[-1])
    proj = unfold(_in_proj(fold(x), sc1, sh1, lp['norm1'], lp['w_in'], lp['qg'], lp['kg'], TOKEN_TILE))
    o_pool = _pool_mixer(proj, lp['w_pool'], lp['pool_scale'])
    if ctx is None:
        o_na = _context_attention(proj)
        s0 = jnp.zeros((b, 2, DN_WIDTH, DN_WIDTH), F32)
    else:
        o_na = _neighbourhood_attention(proj, ctx[0], ctx[1], ctx[3])
        s0 = ctx[2]
    o_dn, s_fin = _deltanet(proj, lp['conv'], lp['alog'], lp['dtb'], lp['dn_norm'], s0)
    x1, h2, gates_t = _out_proj(fold(o_pool), fold(o_na), fold(o_dn), fold(x), lp['w_out'], g1, lp['norm2'],
                                sc2, sh2, w_router_t, b_router, TOKEN_TILE)
    gates = jnp.swapaxes(gates_t, 1, 2)
    x2 = _moe(h2, gates, lp['w_gate'], lp['w_up'], lp['w_down'], x1, g2, MOE_TILE)
    return unfold(x2), proj, s_fin


def kernel(x_prompt, x_sample, cache_na_k, cache_na_v, state_dn, c, c_ctx, norm1_g, norm2_g, w_ada, b_ada, w_in, w_out, w_pool, pool_scale, q_norm_g, k_norm_g, rpb, dn_conv_w, dn_a_log, dn_dt_bias, dn_norm_g, w_router, b_router, w_gate, w_up, w_down):
    depth = w_in.shape[0]
    d = D_MODEL
    bp, sp, _ = x_prompt.shape
    bs, ls, _ = x_sample.shape
    conds = jnp.zeros((8, d), F32).at[0].set(c_ctx).at[1:1 + bs].set(c)
    mods = _ada_mod(conds, w_ada, b_ada).reshape(depth, 8, 6, 1, d)
    w_router_t = w_router.T
    b_col = b_router.reshape(N_EXPERTS, 1)
    heads_to_lanes = lambda t: jnp.swapaxes(t, 1, 2).reshape(t.shape[0], t.shape[2], NA_WIDTH)
    lanes_to_heads = lambda t: jnp.swapaxes(t.reshape(t.shape[0], t.shape[1], NA_HEADS, HEAD_DIM), 1, 2)

    y_prompt, y_sample = x_prompt, x_sample
    new_k, new_v, new_s = [], [], []
    for l in range(depth):
        lp = _layer_params(l, norm1_g, norm2_g, w_in, w_out, w_pool, pool_scale, q_norm_g, k_norm_g, dn_conv_w,
                           dn_a_log, dn_dt_bias, dn_norm_g, w_gate, w_up, w_down)
        y_prompt, proj_p, s_fin = _layer(y_prompt, mods[l, 0:1], lp, w_router_t, b_col, None)
        new_k.append(lanes_to_heads(proj_p[:, :, COL_K:COL_K + NA_WIDTH]))
        new_v.append(lanes_to_heads(proj_p[:, :, COL_V:COL_V + NA_WIDTH]))
        new_s.append(_blockdiag_to_state(s_fin))
        ctx = (heads_to_lanes(cache_na_k[:, l]), heads_to_lanes(cache_na_v[:, l]),
               _state_to_blockdiag(state_dn[:, l]), _na_bias(rpb[l], ls // GRID_W))
        y_sample, _, _ = _layer(y_sample, mods[l, 1:1 + bs], lp, w_router_t, b_col, ctx)
    return (y_prompt, y_sample, jnp.stack(new_k, axis=1), jnp.stack(new_v, axis=1), jnp.stack(new_s, axis=1))
```

```python
import functools
import math

import numpy as np
import jax
import jax.numpy as jnp
from jax import lax
from jax.experimental import pallas as pl
from jax.experimental.pallas import tpu as pltpu

F32 = jnp.float32
BF16 = jnp.bfloat16

D_MODEL = 1024
GRID_W = 64
POOL_WIDTH = 256
POOL_WINDOWS = (2, 4, 8, 16)
POOL_GROUP = 64
NA_WIDTH = 512
HEAD_DIM = 64
NA_HEADS = 8
NA_ROWS = 8
NA_COLS = 16
DN_WIDTH = 256
DN_HEADS = 4
DN_CONV = 4
DN_CHUNK = 64
N_EXPERTS = 16
N_EXPERT_GROUPS = 4
EXPERTS_PER_GROUP = 4
D_EXPERT = 512
EPS = 1e-6
NEG = -1e30

COL_POOL = 0
COL_Q = 256
COL_K = 768
COL_V = 1280
COL_DN = 1792
COL_Z = 2560
COL_BA = 2816
IN_WIDTH = 2832
IN_PAD = 2944

NA_QROWS = 4
NA_KROWS = NA_ROWS + NA_QROWS

VMEM_LIMIT = 56 * 1024 * 1024
TOKEN_TILE = 512
MOE_TILE = 1024


def _cparams(*sem):
    return pltpu.CompilerParams(dimension_semantics=sem, vmem_limit_bytes=VMEM_LIMIT)


def _dot(a, b):
    return jnp.dot(a, b, preferred_element_type=F32)


def _dot_nt(a, b):
    return lax.dot_general(a, b, (((1,), (1,)), ((), ())), preferred_element_type=F32)


def _dot_tn(a, b):
    return lax.dot_general(a, b, (((0,), (0,)), ((), ())), preferred_element_type=F32)


def _split2(a):
    hi = a.astype(BF16)
    lo = (a - hi.astype(F32)).astype(BF16)
    return hi, lo


def _split3(a):
    hi = a.astype(BF16)
    r = a - hi.astype(F32)
    mid = r.astype(BF16)
    lo = (r - mid.astype(F32)).astype(BF16)
    return hi, mid, lo


def _mm(a, b, passes, dotf=_dot):
    if passes == 1:
        return dotf(a.astype(BF16), b.astype(BF16))
    ah, al = _split2(a)
    bh, bl = _split2(b)
    return dotf(ah, bh) + (dotf(ah, bl) + dotf(al, bh))


def _mm_exact_b(a, b01, dotf=_dot):
    a0, a1, a2 = _split3(a)
    return dotf(a0, b01) + (dotf(a1, b01) + dotf(a2, b01))


def _mm_exact_a(a01, b, dotf=_dot):
    b0, b1, b2 = _split3(b)
    return dotf(a01, b0) + (dotf(a01, b1) + dotf(a01, b2))


def _iota(shape, dim):
    return lax.broadcasted_iota(jnp.int32, shape, dim)


def _head_blockdiag(rows, cols):
    return jnp.where(_iota((rows, cols), 0) // HEAD_DIM == _iota((rows, cols), 1) // HEAD_DIM, 1.0, 0.0).astype(BF16)


def _silu(x):
    return x * jax.nn.sigmoid(x)


def _ada_kernel(c_ref, w_ref, b_ref, o_ref):
    o_ref[0] = _mm(_silu(c_ref[...]), w_ref[0], 3) + b_ref[0]


def _ada_mod(conds, w_ada, b_ada):
    depth, d, n = w_ada.shape
    tn = 1536
    return pl.pallas_call(
        _ada_kernel,
        grid=(depth, n // tn),
        in_specs=[pl.BlockSpec((8, d), lambda l, j: (0, 0)),
                  pl.BlockSpec((1, d, tn), lambda l, j: (l, 0, j)),
                  pl.BlockSpec((1, 1, tn), lambda l, j: (l, 0, j))],
        out_specs=pl.BlockSpec((1, 8, tn), lambda l, j: (l, 0, j)),
        out_shape=jax.ShapeDtypeStruct((depth, 8, n), F32),
        compiler_params=_cparams("parallel", "parallel"),
        name="ada_mod",
    )(conds, w_ada, b_ada.reshape(depth, 1, n))


def _inproj_kernel(x_ref, g_ref, sc_ref, sh_ref, w_ref, qg_ref, kg_ref, o_ref):
    x = x_ref[0]
    h = x * lax.rsqrt(jnp.mean(x * x, axis=-1, keepdims=True) + EPS) * g_ref[...]
    h = (h * (1.0 + sc_ref[0]) + sh_ref[0]).astype(BF16)
    bd = _head_blockdiag(256, 256)
    for c0 in range(0, IN_PAD, 256):
        cw = min(256, IN_PAD - c0)
        r = _dot(h, w_ref[:, c0:c0 + cw])
        if COL_Q <= c0 < COL_V:
            gain = qg_ref[...] if c0 < COL_K else kg_ref[...]
            r = r * lax.rsqrt(_mm_exact_b(r * r, bd) * (1.0 / HEAD_DIM) + EPS) * gain
        o_ref[0, :, c0:c0 + cw] = r


def _in_proj(x, mod_sc, mod_sh, norm_g, w_pad, qg, kg, tm):
    b, l, d = x.shape
    bc = mod_sc.shape[0]
    mod_idx = (lambda i, j: (i, 0, 0)) if bc == b else (lambda i, j: (0, 0, 0))
    return pl.pallas_call(
        _inproj_kernel,
        grid=(b, l // tm),
        in_specs=[pl.BlockSpec((1, tm, d), lambda i, j: (i, j, 0)),
                  pl.BlockSpec((1, d), lambda i, j: (0, 0)),
                  pl.BlockSpec((1, 1, d), mod_idx),
                  pl.BlockSpec((1, 1, d), mod_idx),
                  pl.BlockSpec((d, IN_PAD), lambda i, j: (0, 0)),
                  pl.BlockSpec((1, 256), lambda i, j: (0, 0)),
                  pl.BlockSpec((1, 256), lambda i, j: (0, 0))],
        out_specs=pl.BlockSpec((1, tm, IN_PAD), lambda i, j: (i, j, 0)),
        out_shape=jax.ShapeDtypeStruct((b, l, IN_PAD), F32),
        compiler_params=_cparams("parallel", "parallel"),
        name="in_proj",
    )(x, norm_g, mod_sc, mod_sh, w_pad, qg, kg)


def _pool_kernel(u_ref, w_ref, s_ref, o_ref, p1, p2, p4, p8, p16):
    l = u_ref.shape[1]
    p1[0:16] = jnp.zeros((16, POOL_WIDTH), F32)
    p1[16 + l:48 + l] = jnp.zeros((32, POOL_WIDTH), F32)
    u = u_ref[0]
    p1[16:16 + l] = u
    p2[0:l + 40] = p1[0:l + 40] + p1[1:l + 41]
    p4[0:l + 32] = p2[0:l + 32] + p2[2:l + 34]
    p8[0:l + 24] = p4[0:l + 24] + p4[4:l + 28]
    p16[0:l + 16] = p8[0:l + 16] + p8[8:l + 24]
    grp = _iota((l, POOL_WIDTH), 1) // POOL_GROUP
    t = _iota((l, POOL_WIDTH), 0)
    half = jnp.where(grp == 0, 1, jnp.where(grp == 1, 2, jnp.where(grp == 2, 4, 8)))
    cnt = (jnp.minimum(t + half, l) - jnp.maximum(t - half, 0)).astype(F32)
    wsum = jnp.where(grp == 0, p2[15:15 + l],
                     jnp.where(grp == 1, p4[14:14 + l],
                               jnp.where(grp == 2, p8[12:12 + l], p16[8:8 + l])))
    pooled = wsum / cnt - u
    o_ref[0] = _dot(pooled.astype(BF16), w_ref[...]) * s_ref[...]


def _pool_mixer(proj, w_bd, scale):
    b, l, _ = proj.shape
    pad = pltpu.VMEM((l + 48, POOL_WIDTH), F32)
    return pl.pallas_call(
        _pool_kernel,
        grid=(b,),
        in_specs=[pl.BlockSpec((1, l, POOL_WIDTH), lambda i: (i, 0, COL_POOL // POOL_WIDTH)),
                  pl.BlockSpec((POOL_WIDTH, POOL_WIDTH), lambda i: (0, 0)),
                  pl.BlockSpec((1, POOL_WIDTH), lambda i: (0, 0))],
        out_specs=pl.BlockSpec((1, l, POOL_WIDTH), lambda i: (i, 0, 0)),
        out_shape=jax.ShapeDtypeStruct((b, l, POOL_WIDTH), F32),
        scratch_shapes=[pad, pad, pad, pad, pad],
        compiler_params=_cparams("parallel"),
        name="pool_mixer",
    )(proj, w_bd, scale)


def _softmax_pv(scores, values):
    m = functools.reduce(jnp.maximum, [jnp.max(s, axis=-1, keepdims=True) for s in scores])
    ps = [jnp.exp(s - m) for s in scores]
    den = functools.reduce(lambda a, c: a + c, [jnp.sum(p, axis=-1, keepdims=True) for p in ps])
    acc = functools.reduce(lambda a, c: a + c, [_dot(p.astype(BF16), v) for p, v in zip(ps, values)])
    return acc / den


def _ctx_attn_kernel(q_ref, k_ref, v_ref, o_ref):
    q = q_ref[0] * (HEAD_DIM ** -0.5)
    kb = k_ref[0].astype(BF16)
    vb = v_ref[0].astype(BF16)
    lane = _iota(q.shape, 1)
    outs = []
    for h in range(2):
        qm = jnp.where(lane // HEAD_DIM == h, q, 0.0).astype(BF16)
        outs.append(_softmax_pv([_dot_nt(qm, kb)], [vb]))
    o_ref[0] = jnp.where(lane < HEAD_DIM, outs[0], outs[1])


def _context_attention(proj):
    b, s, _ = proj.shape
    npair = NA_HEADS // 2
    blk = lambda col: pl.BlockSpec((1, s, 128), lambda i, p: (i, 0, col // 128 + p))
    return pl.pallas_call(
        _ctx_attn_kernel,
        grid=(b, npair),
        in_specs=[blk(COL_Q), blk(COL_K), blk(COL_V)],
        out_specs=pl.BlockSpec((1, s, 128), lambda i, p: (i, 0, p)),
        out_shape=jax.ShapeDtypeStruct((b, s, NA_WIDTH), F32),
        compiler_params=_cparams("parallel", "parallel"),
        name="context_attention",
    )(proj, proj, proj)


def _na_attn_kernel(q_ref, k_ref, v_ref, kc_ref, vc_ref, bias_ref, o_ref):
    l = q_ref.shape[1]
    rows = l // GRID_W
    ngroups = rows // NA_QROWS
    nq = NA_QROWS * GRID_W
    nk = NA_KROWS * GRID_W
    kc = kc_ref[0].astype(BF16)
    vc = vc_ref[0].astype(BF16)
    lane = _iota((nq, 128), 1)

    def group(g, carry):
        q0 = pl.multiple_of(g * nq, nq)
        k0 = pl.multiple_of(jnp.clip(NA_QROWS * g - NA_ROWS // 2, 0, rows - NA_KROWS) * GRID_W, GRID_W)
        pat = jnp.where(g == 0, 0, jnp.where(g == ngroups - 1, 2, 1))
        q = q_ref[0, pl.ds(q0, nq), :] * (HEAD_DIM ** -0.5)
        kl = k_ref[0, pl.ds(k0, nk), :].astype(BF16)
        vl = v_ref[0, pl.ds(k0, nk), :].astype(BF16)
        outs = []
        for h in range(2):
            qm = jnp.where(lane // HEAD_DIM == h, q, 0.0).astype(BF16)
            s_loc = _dot_nt(qm, kl) + bias_ref[h, pat]
            s_ctx = _dot_nt(qm, kc)
            outs.append(_softmax_pv([s_loc, s_ctx], [vl, vc]))
        o_ref[0, pl.ds(q0, nq), :] = jnp.where(lane < HEAD_DIM, outs[0], outs[1])
        return carry

    lax.fori_loop(0, ngroups, group, 0)


def _na_bias(rpb, rows):
    assert rows >= 2 * NA_KROWS - NA_ROWS and rows % NA_QROWS == 0
    ngroups = rows // NA_QROWS
    col = np.arange(GRID_W)
    cs = np.clip(col - NA_COLS // 2, 0, GRID_W - NA_COLS)
    in_win = (col[None, :] >= cs[:, None]) & (col[None, :] < cs[:, None] + NA_COLS)
    nh, ndr, ndc = rpb.shape
    padded = jnp.pad(rpb.astype(F32), ((0, 0), (0, 0), (GRID_W, GRID_W)))
    tiles = jnp.stack([padded[:, :, GRID_W + NA_COLS - 1 - qc:2 * GRID_W + NA_COLS - 1 - qc] for qc in range(GRID_W)],
                      axis=2)
    tiles = jnp.where(in_win[None, None], tiles, NEG)
    masked = jnp.full((nh, GRID_W, GRID_W), NEG, F32)
    pats = []
    for g in (0, 1, ngroups - 1):
        u = int(np.clip(NA_QROWS * g - NA_ROWS // 2, 0, rows - NA_KROWS))
        qrows = []
        for a in range(NA_QROWS):
            r = NA_QROWS * g + a
            r0 = int(np.clip(r - NA_ROWS // 2, 0, rows - NA_ROWS))
            krow = [tiles[:, u + j - r + NA_ROWS - 1] if r0 <= u + j < r0 + NA_ROWS else masked
                    for j in range(NA_KROWS)]
            qrows.append(jnp.stack(krow, axis=2))
        pats.append(jnp.stack(qrows, axis=1))
    return jnp.stack(pats, axis=1).reshape(nh, 3, NA_QROWS * GRID_W, NA_KROWS * GRID_W)


def _neighbourhood_attention(proj, k_ctx, v_ctx, bias):
    b, l, _ = proj.shape
    p = k_ctx.shape[1]
    npair = NA_HEADS // 2
    nq, nk = bias.shape[2], bias.shape[3]
    blk = lambda col: pl.BlockSpec((1, l, 128), lambda pr, i: (i, 0, col // 128 + pr))
    ctx = pl.BlockSpec((1, p, 128), lambda pr, i: (i, 0, pr))
    return pl.pallas_call(
        _na_attn_kernel,
        grid=(npair, b),
        in_specs=[blk(COL_Q), blk(COL_K), blk(COL_V), ctx, ctx,
                  pl.BlockSpec((2, 3, nq, nk), lambda pr, i: (pr, 0, 0, 0))],
        out_specs=pl.BlockSpec((1, l, 128), lambda pr, i: (i, 0, pr)),
        out_shape=jax.ShapeDtypeStruct((b, l, NA_WIDTH), F32),
        compiler_params=_cparams("parallel", "parallel"),
        name="neighbourhood_attention",
    )(proj, proj, proj, k_ctx, v_ctx, bias)


DN_INV_PASSES = 3
DN_PRE_TILE = 256


def _tile4(x):
    return jnp.concatenate([x, x, x, x], axis=0)


def _dn_consts(d):
    c, w = DN_CHUNK, DN_WIDTH
    pos = _iota((c, w), 1) % c
    row = _iota((c, w), 0)
    i64 = _iota((c, c), 0)
    j64 = _iota((c, c), 1)
    one = lambda m: jnp.where(m, 1.0, 0.0).astype(BF16)
    before = (lambda a, b: a <= b) if d == 0 else (lambda a, b: a >= b)
    sel_r = _iota((128, w), 0)
    sel_h = _iota((128, w), 1) // HEAD_DIM
    return dict(
        incl=before(pos, row),
        strict=before(pos, row) & (pos != row),
        gmask=before(row, pos),
        eye=jnp.where(pos == row, 1.0, 0.0),
        cum=jnp.concatenate([one(before(j64, i64)), jnp.ones((c, c), BF16)], axis=0),
        ones=jnp.ones((c, c), BF16),
        e_beta=one(sel_r == DN_HEADS * d + sel_h),
        e_g=one(sel_r == 2 * DN_HEADS + DN_HEADS * d + sel_h))


def _dn_kernel(qr_ref, kr_ref, vr_ref, z_ref, ba_ref, cw_ref, alog_ref, dtb_ref, ng_ref, s0_ref,
               o_ref, sfin_ref, xpad, qs, ks, vs, od, sbd):
    l = qr_ref.shape[1]
    nc = l // DN_CHUNK
    c = DN_CHUNK
    w = DN_WIDTH
    tr = min(l, DN_PRE_TILE)

    for i, (src, dst) in enumerate(((qr_ref, qs), (kr_ref, ks), (vr_ref, vs))):
        xpad[0:8] = jnp.zeros((8, w), F32)
        xpad[8 + l:16 + l] = jnp.zeros((8, w), F32)
        xpad[8:8 + l] = src[0]
        bd = _head_blockdiag(w, w)
        for r0 in range(0, l, tr):
            acc = cw_ref[0:1, i * w:(i + 1) * w] * xpad[r0 + 7:r0 + 7 + tr]
            for j in range(1, DN_CONV):
                acc = acc + cw_ref[j:j + 1, i * w:(i + 1) * w] * xpad[r0 + 7 + j:r0 + 7 + j + tr]
            y = _silu(acc)
            if i == 0:
                y = y * lax.rsqrt(_mm_exact_b(y * y, bd) + EPS) * (HEAD_DIM ** -0.5)
            elif i == 1:
                y = y * lax.rsqrt(_mm_exact_b(y * y, bd) + EPS)
            dst[r0:r0 + tr] = y

    sbd[0] = s0_ref[0, 0]
    sbd[1] = s0_ref[0, 1]

    def chunk_step(c0, d):
        cst = _dn_consts(d)
        bdm = _head_blockdiag(w, w) > 0
        bdiag = lambda x: jnp.where(bdm, _tile4(x), 0.0)
        qc = qs[pl.ds(c0, c)]
        kc = ks[pl.ds(c0, c)]
        vc = vs[pl.ds(c0, c)]
        ba = ba_ref[0, pl.ds(c0, c), :]
        beta = _mm_exact_b(jax.nn.sigmoid(ba), cst["e_beta"])
        x = ba + dtb_ref[...]
        softplus = jnp.maximum(x, 0.0) + jnp.log1p(jnp.exp(-jnp.abs(x)))
        g = _mm_exact_b(-jnp.exp(alog_ref[...]) * softplus, cst["e_g"])
        gg = _mm_exact_a(cst["cum"], g)
        gcum, gtot = gg[0:c], gg[c:2 * c]
        grow = _mm_exact_a(cst["ones"], jnp.where(cst["gmask"], g, 0.0))
        incl = cst["incl"]
        decay = jnp.where(incl, jnp.exp(jnp.where(incl, gcum - grow, 0.0)), 0.0)
        eg = jnp.exp(gcum)
        kb = kc * beta
        kq = _dot_nt(jnp.concatenate([kb, qc], axis=0).astype(BF16), bdiag(kc).astype(BF16))
        nm = jnp.where(cst["strict"], kq[0:c] * decay, 0.0)
        attn = kq[c:2 * c] * decay
        t = cst["eye"] - nm
        p = _mm(nm, bdiag(nm), DN_INV_PASSES)
        for _ in range(4):
            tp = _mm(jnp.concatenate([t, p], axis=0), bdiag(p), DN_INV_PASSES)
            t = t + tp[0:c]
            p = tp[c:2 * c]
        t = t + _mm(t, bdiag(p), DN_INV_PASSES)
        u = _mm(t, bdiag(vc * beta), DN_INV_PASSES)
        wmat = _mm(t, bdiag(kb * eg), DN_INV_PASSES)
        s = sbd[d]
        ws = _mm(jnp.concatenate([wmat, qc * eg], axis=0), s, 1)
        vnew = u - ws[0:c]
        o = ws[c:2 * c] + _mm(attn, bdiag(vnew), 1)
        kdec = kc * jnp.exp(gtot - gcum)
        upd = _dot_tn(kdec.astype(BF16), vnew.astype(BF16))
        sbd[d] = s * jnp.exp(gtot[0:1]) + jnp.where(bdm, upd, 0.0)
        od[d, pl.ds(c0, c)] = o

    def body(n, carry):
        chunk_step(pl.multiple_of(n * c, c), 0)
        chunk_step(pl.multiple_of((nc - 1 - n) * c, c), 1)
        return carry

    lax.fori_loop(0, nc, body, 0)
    sfin_ref[0, 0] = sbd[0]
    sfin_ref[0, 1] = sbd[1]

    bd = _head_blockdiag(w, w)
    for r0 in range(0, l, tr):
        o = od[0, r0:r0 + tr] + od[1, r0:r0 + tr]
        o = o * lax.rsqrt(_mm_exact_b(o * o, bd) * (1.0 / HEAD_DIM) + EPS) * ng_ref[...]
        o_ref[0, r0:r0 + tr] = o * _silu(z_ref[0, r0:r0 + tr])


def _deltanet(proj, conv_w, alog_row, dtb_row, norm_row, s0_bd):
    b, l, _ = proj.shape
    w = DN_WIDTH
    col = lambda c0: pl.BlockSpec((1, l, w), lambda i: (i, 0, c0 // w))
    return pl.pallas_call(
        _dn_kernel,
        grid=(b,),
        in_specs=[col(COL_DN), col(COL_DN + w), col(COL_DN + 2 * w), col(COL_Z),
                  pl.BlockSpec((1, l, 128), lambda i: (i, 0, COL_BA // 128)),
                  pl.BlockSpec((DN_CONV, 3 * w), lambda i: (0, 0)),
                  pl.BlockSpec((1, 128), lambda i: (0, 0)),
                  pl.BlockSpec((1, 128), lambda i: (0, 0)),
                  pl.BlockSpec((1, w), lambda i: (0, 0)),
                  pl.BlockSpec((1, 2, w, w), lambda i: (i, 0, 0, 0))],
        out_specs=[pl.BlockSpec((1, l, w), lambda i: (i, 0, 0)),
                   pl.BlockSpec((1, 2, w, w), lambda i: (i, 0, 0, 0))],
        out_shape=[jax.ShapeDtypeStruct((b, l, w), F32),
                   jax.ShapeDtypeStruct((b, 2, w, w), F32)],
        scratch_shapes=[pltpu.VMEM((l + 16, w), F32),
                        pltpu.VMEM((l, w), F32), pltpu.VMEM((l, w), F32), pltpu.VMEM((l, w), F32),
                        pltpu.VMEM((2, l, w), F32),
                        pltpu.VMEM((2, w, w), F32)],
        compiler_params=_cparams("parallel"),
        name="deltanet",
    )(proj, proj, proj, proj, proj, conv_w, alog_row, dtb_row, norm_row, s0_bd)


def _top2_sum(a, b, c, d):
    hi1, lo1 = jnp.maximum(a, b), jnp.minimum(a, b)
    hi2, lo2 = jnp.maximum(c, d), jnp.minimum(c, d)
    return jnp.maximum(hi1, hi2) + jnp.maximum(jnp.minimum(hi1, hi2), jnp.maximum(lo1, lo2))


def _outproj_kernel(op_ref, on_ref, od_ref, x_ref, w_ref, g1_ref, n2_ref, sc_ref, sh_ref, wr_ref, br_ref,
                    x1_ref, h2_ref, gt_ref):
    mix = (_dot(op_ref[0].astype(BF16), w_ref[0:POOL_WIDTH])
           + _dot(on_ref[0].astype(BF16), w_ref[POOL_WIDTH:POOL_WIDTH + NA_WIDTH])
           + _dot(od_ref[0].astype(BF16), w_ref[POOL_WIDTH + NA_WIDTH:D_MODEL]))
    x1 = x_ref[0] + g1_ref[0] * mix
    x1_ref[0] = x1
    h2 = x1 * lax.rsqrt(jnp.mean(x1 * x1, axis=-1, keepdims=True) + EPS) * n2_ref[...]
    h2 = h2 * (1.0 + sc_ref[0]) + sh_ref[0]
    h2_ref[0] = h2.astype(BF16)
    scores = jax.nn.sigmoid(_mm(wr_ref[...], h2, 3, _dot_nt))
    sel = scores + br_ref[...]
    rows = [sel[e:e + 1] for e in range(N_EXPERTS)]
    gsum = [_top2_sum(*rows[EXPERTS_PER_GROUP * g:EXPERTS_PER_GROUP * (g + 1)]) for g in range(N_EXPERT_GROUPS)]
    best = jnp.zeros_like(gsum[0], dtype=jnp.int32)
    bestv = gsum[0]
    for g in range(1, N_EXPERT_GROUPS):
        better = gsum[g] > bestv
        best = jnp.where(better, g, best)
        bestv = jnp.where(better, gsum[g], bestv)
    picked = []
    for e in range(N_EXPERTS):
        g = e // EXPERTS_PER_GROUP
        rank = jnp.zeros_like(best)
        for o in range(EXPERTS_PER_GROUP * g, EXPERTS_PER_GROUP * (g + 1)):
            if o != e:
                ahead = (rows[o] > rows[e]) if o > e else (rows[o] >= rows[e])
                rank = rank + ahead.astype(jnp.int32)
        picked.append(jnp.where((best == g) & (rank < 2), scores[e:e + 1], 0.0))
    den = functools.reduce(lambda a, c: a + c, picked)
    for e in range(N_EXPERTS):
        gt_ref[0, e:e + 1, :] = picked[e] / den


def _out_proj(o_pool, o_na, o_dn, x, w_out, g1, norm_g, sc2, sh2, w_router_t, b_router, tm):
    b, l, d = x.shape
    bc = g1.shape[0]
    mod_idx = (lambda i, j: (i, 0, 0)) if bc == b else (lambda i, j: (0, 0, 0))
    tok = lambda width: pl.BlockSpec((1, tm, width), lambda i, j: (i, j, 0))
    full = lambda a: pl.BlockSpec(a.shape, lambda i, j: tuple(0 for _ in a.shape))
    mod = pl.BlockSpec((1, 1, d), mod_idx)
    return pl.pallas_call(
        _outproj_kernel,
        grid=(b, l // tm),
        in_specs=[tok(POOL_WIDTH), tok(NA_WIDTH), tok(DN_WIDTH), tok(d), full(w_out), mod, full(norm_g), mod, mod,
                  full(w_router_t), full(b_router)],
        out_specs=[tok(d), tok(d), pl.BlockSpec((1, N_EXPERTS, tm), lambda i, j: (i, 0, j))],
        out_shape=[jax.ShapeDtypeStruct((b, l, d), F32),
                   jax.ShapeDtypeStruct((b, l, d), BF16),
                   jax.ShapeDtypeStruct((b, N_EXPERTS, l), F32)],
        compiler_params=_cparams("parallel", "parallel"),
        name="out_proj",
    )(o_pool, o_na, o_dn, x, w_out, g1, norm_g, sc2, sh2, w_router_t, b_router)


def _moe_kernel(h_ref, gates_ref, wg_ref, wu_ref, wd_ref, x1_ref, g2_ref, o_ref, acc_ref):
    e = pl.program_id(2)

    @pl.when(e == 0)
    def _():
        acc_ref[...] = jnp.zeros_like(acc_ref)

    h = h_ref[0]
    gates = gates_ref[0]
    gate = jnp.sum(jnp.where(_iota(gates.shape, 1) == e, gates, 0.0), axis=-1, keepdims=True)
    hid = _silu(_dot(h, wg_ref[0])) * _dot(h, wu_ref[0]) * gate
    acc_ref[...] += _dot(hid.astype(BF16), wd_ref[0])

    @pl.when(e == N_EXPERTS - 1)
    def _():
        o_ref[0] = x1_ref[0] + g2_ref[0] * acc_ref[...]


def _moe(h2, gates, w_gate, w_up, w_down, x1, g2, tm):
    b, l, d = x1.shape
    bc = g2.shape[0]
    mod_idx = (lambda i, j, e: (i, 0, 0)) if bc == b else (lambda i, j, e: (0, 0, 0))
    tok = lambda width: pl.BlockSpec((1, tm, width), lambda i, j, e: (i, j, 0))
    return pl.pallas_call(
        _moe_kernel,
        grid=(b, l // tm, N_EXPERTS),
        in_specs=[tok(d), tok(N_EXPERTS),
                  pl.BlockSpec((1, d, D_EXPERT), lambda i, j, e: (e, 0, 0)),
                  pl.BlockSpec((1, d, D_EXPERT), lambda i, j, e: (e, 0, 0)),
                  pl.BlockSpec((1, D_EXPERT, d), lambda i, j, e: (e, 0, 0)),
                  tok(d), pl.BlockSpec((1, 1, d), mod_idx)],
        out_specs=tok(d),
        out_shape=jax.ShapeDtypeStruct((b, l, d), F32),
        scratch_shapes=[pltpu.VMEM((tm, d), F32)],
        compiler_params=_cparams("parallel", "parallel", "arbitrary"),
        name="moe",
    )(h2, gates, w_gate, w_up, w_down, x1, g2)


def _layer_params(l, norm1_g, norm2_g, w_in, w_out, w_pool, pool_scale, q_norm_g, k_norm_g, dn_conv_w, dn_a_log,
                  dn_dt_bias, dn_norm_g, w_gate, w_up, w_down):
    d = D_MODEL
    row128 = lambda v: jnp.zeros((1, 128), F32).at[0, 2 * DN_HEADS:4 * DN_HEADS].set(v.reshape(-1).astype(F32))
    w_bd = jnp.zeros((POOL_WIDTH, POOL_WIDTH), F32)
    for g in range(len(POOL_WINDOWS)):
        sl = slice(g * POOL_GROUP, (g + 1) * POOL_GROUP)
        w_bd = w_bd.at[sl, sl].set(w_pool[l, g])
    return dict(
        norm1=norm1_g[l].reshape(1, d), norm2=norm2_g[l].reshape(1, d),
        w_in=jnp.pad(w_in[l], ((0, 0), (0, IN_PAD - IN_WIDTH))).astype(BF16),
        w_out=w_out[l].astype(BF16),
        w_pool=w_bd.astype(BF16), pool_scale=pool_scale[l].reshape(1, POOL_WIDTH),
        qg=jnp.tile(q_norm_g[l], 256 // HEAD_DIM).reshape(1, 256),
        kg=jnp.tile(k_norm_g[l], 256 // HEAD_DIM).reshape(1, 256),
        conv=dn_conv_w[l], alog=row128(dn_a_log[l]), dtb=row128(dn_dt_bias[l]),
        dn_norm=jnp.tile(dn_norm_g[l], DN_HEADS).reshape(1, DN_WIDTH),
        w_gate=w_gate[l].astype(BF16), w_up=w_up[l].astype(BF16), w_down=w_down[l].astype(BF16))


def _state_to_blockdiag(s):
    b = s.shape[0]
    eye = jnp.eye(DN_HEADS, dtype=s.dtype)
    return (s[:, :, :, :, None, :] * eye[None, None, :, None, :, None]).reshape(b, 2, DN_WIDTH, DN_WIDTH)


def _blockdiag_to_state(s):
    b = s.shape[0]
    s6 = s.reshape(b, 2, DN_HEADS, HEAD_DIM, DN_HEADS, HEAD_DIM)
    return jnp.stack([s6[:, :, h, :, h, :] for h in range(DN_HEADS)], axis=2)


def _layer(x, mod, lp, w_router_t, b_router, ctx):
    b, l, d = x.shape
    sh1, sc1, g1, sh2, sc2, g2 = [mod[:, i] for i in range(6)]
    fold = (lambda t: t.reshape(1, b * l, t.shape[-1])) if mod.shape[0] == 1 else (lambda t: t)
    unfold = lambda t: t.reshape(b, l, t.shape[-1])
    proj = unfold(_in_proj(fold(x), sc1, sh1, lp['norm1'], lp['w_in'], lp['qg'], lp['kg'], TOKEN_TILE))
    o_pool = _pool_mixer(proj, lp['w_pool'], lp['pool_scale'])
    if ctx is None:
        o_na = _context_attention(proj)
        s0 = jnp.zeros((b, 2, DN_WIDTH, DN_WIDTH), F32)
    else:
        o_na = _neighbourhood_attention(proj, ctx[0], ctx[1], ctx[3])
        s0 = ctx[2]
    o_dn, s_fin = _deltanet(proj, lp['conv'], lp['alog'], lp['dtb'], lp['dn_norm'], s0)
    x1, h2, gates_t = _out_proj(fold(o_pool), fold(o_na), fold(o_dn), fold(x), lp['w_out'], g1, lp['norm2'],
                                sc2, sh2, w_router_t, b_router, TOKEN_TILE)
    gates = jnp.swapaxes(gates_t, 1, 2)
    x2 = _moe(h2, gates, lp['w_gate'], lp['w_up'], lp['w_down'], x1, g2, MOE_TILE)
    return unfold(x2), proj, s_fin


def kernel(x_prompt, x_sample, cache_na_k, cache_na_v, state_dn, c, c_ctx, norm1_g, norm2_g, w_ada, b_ada, w_in, w_out, w_pool, pool_scale, q_norm_g, k_norm_g, rpb, dn_conv_w, dn_a_log, dn_dt_bias, dn_norm_g, w_router, b_router, w_gate, w_up, w_down):
    depth = w_in.shape[0]
    d = D_MODEL
    bp, sp, _ = x_prompt.shape
    bs, ls, _ = x_sample.shape
    conds = jnp.zeros((8, d), F32).at[0].set(c_ctx).at[1:1 + bs].set(c)
    mods = _ada_mod(conds, w_ada, b_ada).reshape(depth, 8, 6, 1, d)
    w_router_t = w_router.T
    b_col = b_router.reshape(N_EXPERTS, 1)
    heads_to_lanes = lambda t: jnp.swapaxes(t, 1, 2).reshape(t.shape[0], t.shape[2], NA_WIDTH)
    lanes_to_heads = lambda t: jnp.swapaxes(t.reshape(t.shape[0], t.shape[1], NA_HEADS, HEAD_DIM), 1, 2)

    y_prompt, y_sample = x_prompt, x_sample
    new_k, new_v, new_s = [], [], []
    for l in range(depth):
        lp = _layer_params(l, norm1_g, norm2_g, w_in, w_out, w_pool, pool_scale, q_norm_g, k_norm_g, dn_conv_w,
                           dn_a_log, dn_dt_bias, dn_norm_g, w_gate, w_up, w_down)
        y_prompt, proj_p, s_fin = _layer(y_prompt, mods[l, 0:1], lp, w_router_t, b_col, None)
        new_k.append(lanes_to_heads(proj_p[:, :, COL_K:COL_K + NA_WIDTH]))
        new_v.append(lanes_to_heads(proj_p[:, :, COL_V:COL_V + NA_WIDTH]))
        new_s.append(_blockdiag_to_state(s_fin))
        ctx = (heads_to_lanes(cache_na_k[:, l]), heads_to_lanes(cache_na_v[:, l]),
               _state_to_blockdiag(state_dn[:, l]), _na_bias(rpb[l], ls // GRID_W))
        y_sample, _, _ = _layer(y_sample, mods[l, 1:1 + bs], lp, w_router_t, b_col, ctx)
    return (y_prompt, y_sample, jnp.stack(new_k, axis=1), jnp.stack(new_v, axis=1), jnp.stack(new_s, axis=1))
```
